```python
import math
import jax
import jax.numpy as jnp
from jax import lax
import numpy as np

D_MODEL = 2048
BATCH = 4
SEQ = 2048
DEPTH = 4
DEC_BATCH = 8
DEC_SEQ = 4
PAST_LEN = 16384
PAGE_SIZE = 128

W_A = D_MODEL // 4
CONV_K = 31
HD_B = 128
W_B = D_MODEL // 4
HB = W_B // HD_B
DIL_CONFIGS = ((128, 1), (512, 4), (2048, 16))
N_DIL = len(DIL_CONFIGS)
Q_BLOCK = 128
W_C = D_MODEL // 2
HC = 4
DV_C = W_C // HC
DK_C = DV_C // 2
RET_CHUNK = 128
D_MIX = W_A + W_B + W_C
D_IN = 2 * W_A + 3 * N_DIL * W_B + 2 * HC * DK_C + 2 * W_C
D_FF = 4 * D_MODEL
ROPE_THETA = 10000.0
ALPHA = (2 * DEPTH) ** 0.25
BETA = (8 * DEPTH) ** -0.25
LN_EPS = 1e-5
NEG_INF = -1e30

kernel_name = 'hymba_conv_dilated_retention_decode_step'


def layer_norm(x, g=None, b=None):
    x32 = x.astype(jnp.float32)
    mu = x32.mean(-1, keepdims=True)
    var = jnp.square(x32 - mu).mean(-1, keepdims=True)
    y = (x32 - mu) * lax.rsqrt(var + LN_EPS)
    if g is not None:
        y = y * g.astype(jnp.float32) + b.astype(jnp.float32)
    return y.astype(x.dtype)


def rope(x, pos):
    half = x.shape[-1] // 2
    inv = ROPE_THETA ** (-jnp.arange(half, dtype=jnp.float32) / half)
    ang = pos.astype(jnp.float32)[:, None] * inv[None, :]
    cos = jnp.cos(ang)[:, None, :]
    sin = jnp.sin(ang)[:, None, :]
    x32 = x.astype(jnp.float32)
    x1, x2 = x32[..., :half], x32[..., half:]
    return jnp.concatenate([x1 * cos - x2 * sin, x2 * cos + x1 * sin], axis=-1).astype(x.dtype)


def split_proj(z):
    sizes = (W_A, W_A, N_DIL * W_B, N_DIL * W_B, N_DIL * W_B, HC * DK_C, HC * DK_C, W_C, W_C)
    return jnp.split(z, np.cumsum(sizes)[:-1].tolist(), axis=-1)


def project_heads(z, pos):
    a, ga, qb, kb, vb, qc, kc, vc, gc = split_proj(z)
    B_, T = z.shape[0], z.shape[1]
    u = a * jax.nn.sigmoid(ga)
    qb = rope(qb.reshape(B_, T, N_DIL * HB, HD_B), pos).reshape(B_, T, N_DIL, HB, HD_B)
    kb = rope(kb.reshape(B_, T, N_DIL * HB, HD_B), pos).reshape(B_, T, N_DIL, HB, HD_B)
    vb = vb.reshape(B_, T, N_DIL, HB, HD_B)
    qc = rope(qc.reshape(B_, T, HC, DK_C), pos).astype(jnp.float32)
    kc = rope(kc.reshape(B_, T, HC, DK_C), pos).astype(jnp.float32) * (DK_C ** -0.5)
    vc = vc.reshape(B_, T, HC, DV_C).astype(jnp.float32)
    return u, qb, kb, vb, qc, kc, vc, gc


def causal_dwconv(u_ext, w, b):
    y = lax.conv_general_dilated(u_ext, w[:, None, :].astype(u_ext.dtype), (1,), 'VALID',
                                 dimension_numbers=('NWC', 'WIO', 'NWC'),
                                 feature_group_count=u_ext.shape[-1])
    return y + b


def conv_tail(y, g, b):
    return jax.nn.silu(layer_norm(y, g, b))


def dilated_prompt(q, k, v, window, dil):
    B_, T, H, hd = q.shape
    nw = window // dil
    S = T // dil
    nblk = -(-S // Q_BLOCK)
    nprev = -(-nw // Q_BLOCK)
    pad_r = nblk * Q_BLOCK - S

    def streams(x):
        return x.reshape(B_, S, dil, H, hd).transpose(0, 2, 1, 3, 4)

    qs = jnp.pad(streams(q), ((0, 0), (0, 0), (0, pad_r), (0, 0), (0, 0)))
    qs = qs.reshape(B_, dil, nblk, Q_BLOCK, H, hd)

    def key_blocks(x):
        xp = jnp.pad(streams(x), ((0, 0), (0, 0), (nprev * Q_BLOCK, pad_r), (0, 0), (0, 0)))
        xp = xp.reshape(B_, dil, nblk + nprev, Q_BLOCK, H, hd)
        return jnp.concatenate([xp[:, :, j:j + nblk] for j in range(nprev + 1)], axis=3)

    kbk, vbk = key_blocks(k), key_blocks(v)
    qi = jnp.arange(Q_BLOCK)
    ki = jnp.arange((nprev + 1) * Q_BLOCK)
    blk = jnp.arange(nblk)
    k_idx = blk[:, None, None] * Q_BLOCK + ki[None, None, :] - nprev * Q_BLOCK
    dist = (qi[None, :, None] + nprev * Q_BLOCK) - ki[None, None, :]
    mask = (dist >= 0) & (dist <= nw) & (k_idx >= 0)
    s = jnp.einsum('bgnqhd,bgnkhd->bgnhqk', qs, kbk).astype(jnp.float32) * (hd ** -0.5)
    s = jnp.where(mask[None, None, :, None], s, NEG_INF)
    m = s.max(-1, keepdims=True)
    p = jnp.exp(s - m)
    den = p.sum(-1, keepdims=True)
    o = jnp.einsum('bgnhqk,bgnkhd->bgnqhd', p / den, vbk.astype(jnp.float32))
    lse = (m + jnp.log(den))[..., 0]
    o = o.reshape(B_, dil, nblk * Q_BLOCK, H, hd)[:, :, :S].transpose(0, 2, 1, 3, 4).reshape(B_, T, H, hd)
    lse = lse.transpose(0, 1, 2, 4, 3).reshape(B_, dil, nblk * Q_BLOCK, H)[:, :, :S]
    lse = lse.transpose(0, 2, 1, 3).reshape(B_, T, H)
    return o.astype(q.dtype), lse


def dilated_sample(q, k_all, v_all, window, dil):
    Tn, hd = q.shape[1], q.shape[-1]
    L = k_all.shape[1] - Tn
    nw = window // dil
    idx = L + jnp.arange(Tn)[:, None] - dil * jnp.arange(nw + 1)[None, :]
    valid = idx >= 0
    idx = jnp.maximum(idx, 0)
    kg = k_all[:, idx]
    vg = v_all[:, idx]
    s = jnp.einsum('bqhd,bqjhd->bhqj', q, kg).astype(jnp.float32) * (hd ** -0.5)
    s = jnp.where(valid[None, None], s, NEG_INF)
    m = s.max(-1, keepdims=True)
    p = jnp.exp(s - m)
    den = p.sum(-1, keepdims=True)
    o = jnp.einsum('bhqj,bqjhd->bqhd', p / den, vg.astype(jnp.float32))
    lse = (m + jnp.log(den))[..., 0].transpose(0, 2, 1)
    return o.astype(q.dtype), lse


def merge_dilations(outs, lses):
    wts = jax.nn.softmax(jnp.stack(lses, 0), axis=0)
    o = jnp.einsum('gbth,gbthd->bthd', wts, jnp.stack(outs, 0).astype(jnp.float32))
    return o.reshape(o.shape[0], o.shape[1], W_B).astype(outs[0].dtype)


def retention_log_decay():
    return jnp.log1p(-jnp.exp2(-5.0 - jnp.arange(HC, dtype=jnp.float32)))


def retention_chunk(S, q, k, v, log_gamma):
    C = q.shape[1]
    n = jnp.arange(C, dtype=jnp.float32)
    diff = n[:, None] - n[None, :]
    decay = jnp.where(diff >= 0, jnp.exp(log_gamma[:, None, None] * jnp.maximum(diff, 0.0)), 0.0)
    inner = jnp.exp(log_gamma[None, :] * (n[:, None] + 1.0))
    kdec = jnp.exp(log_gamma[None, :] * (C - 1.0 - n[:, None]))
    cdec = jnp.exp(log_gamma * C)
    sc = jnp.einsum('bnhd,bmhd->bhnm', q, k) * decay[None]
    o = jnp.einsum('bhnm,bmhe->bnhe', sc, v) + jnp.einsum('bnhd,bhde->bnhe', q, S) * inner[None, :, :, None]
    S_new = S * cdec[None, :, None, None] + jnp.einsum('bmhd,bmhe->bhde', k * kdec[None, :, :, None], v)
    return S_new, o


def retention_prompt(q, k, v, log_gamma):
    B_, T = q.shape[0], q.shape[1]
    nc = T // RET_CHUNK

    def to_chunks(x):
        return x.reshape(B_, nc, RET_CHUNK, x.shape[2], x.shape[3]).swapaxes(0, 1)

    S0 = jnp.zeros((B_, HC, DK_C, DV_C), jnp.float32)
    S, o = lax.scan(lambda St, inp: retention_chunk(St, inp[0], inp[1], inp[2], log_gamma),
                    S0, (to_chunks(q), to_chunks(k), to_chunks(v)))
    o = o.swapaxes(0, 1).reshape(B_, T, HC, DV_C)
    return o, S


def retention_out(o, gate):
    B_, T = o.shape[0], o.shape[1]
    gn = layer_norm(o).reshape(B_, T, W_C)
    return jax.nn.silu(gate) * gn.astype(gate.dtype)


def mix_prompt(z, pos, conv_w_l, conv_b_l, conv_g_l, conv_beta_l, log_gamma):
    u, qb, kb, vb, qc, kc, vc, gc = project_heads(z, pos)
    T = z.shape[1]
    u_ext = jnp.pad(u, ((0, 0), (CONV_K - 1, 0), (0, 0)))
    ya = conv_tail(causal_dwconv(u_ext, conv_w_l, conv_b_l), conv_g_l, conv_beta_l)
    outs, lses, wins = [], [], []
    for g, (window, dil) in enumerate(DIL_CONFIGS):
        o, lse = dilated_prompt(qb[:, :, g], kb[:, :, g], vb[:, :, g], window, dil)
        outs.append(o)
        lses.append(lse)
        L = min(window, T)
        wins.append(jnp.stack([kb[:, T - L:, g], vb[:, T - L:, g]], axis=2))
    yb = merge_dilations(outs, lses)
    oc, S = retention_prompt(qc, kc, vc, log_gamma)
    yc = retention_out(oc, gc)
    y = jnp.concatenate([ya, yb, yc], axis=-1)
    return y, (u_ext[:, T:], wins, S)


def mix_sample(z, pos, conv_buf, win_bufs, S, conv_w_l, conv_b_l, conv_g_l, conv_beta_l, log_gamma):
    u, qb, kb, vb, qc, kc, vc, gc = project_heads(z, pos)
    Tn = z.shape[1]
    u_ext = jnp.concatenate([conv_buf.astype(u.dtype), u], axis=1)
    ya = conv_tail(causal_dwconv(u_ext, conv_w_l, conv_b_l), conv_g_l, conv_beta_l)
    outs, lses, wins = [], [], []
    for g, (window, dil) in enumerate(DIL_CONFIGS):
        buf = win_bufs[g].astype(kb.dtype)
        k_all = jnp.concatenate([buf[:, :, 0], kb[:, :, g]], axis=1)
        v_all = jnp.concatenate([buf[:, :, 1], vb[:, :, g]], axis=1)
        o, lse = dilated_sample(qb[:, :, g], k_all, v_all, window, dil)
        outs.append(o)
        lses.append(lse)
        wins.append(jnp.stack([k_all[:, Tn:], v_all[:, Tn:]], axis=2))
    yb = merge_dilations(outs, lses)
    S_new, oc = retention_chunk(S.astype(jnp.float32), qc, kc, vc, log_gamma)
    yc = retention_out(oc, gc)
    y = jnp.concatenate([ya, yb, yc], axis=-1)
    return y, (u_ext[:, Tn:], wins, S_new)


def decoder_layer(x, c, mix_fn, w_ada_l, b_ada_l, w_in_l, w_o_l, ln1_g_l, ln1_b_l,
                  w_up_l, w_down_l, ln2_g_l, ln2_b_l):
    mod = jax.nn.silu(c) @ w_ada_l + b_ada_l
    sh1, sc1, g1, sh2, sc2, g2 = jnp.split(mod, 6, axis=-1)
    h = x * (1.0 + sc1[:, None]) + sh1[:, None]
    y, states = mix_fn(h @ w_in_l)
    x = layer_norm(ALPHA * x + g1[:, None] * (y @ w_o_l), ln1_g_l, ln1_b_l)
    h = x * (1.0 + sc2[:, None]) + sh2[:, None]
    f = jnp.square(jax.nn.relu(h @ w_up_l)) @ w_down_l
    x = layer_norm(ALPHA * x + g2[:, None] * f, ln2_g_l, ln2_b_l)
    return x, states


def setup_inputs(seed: int = 0) -> dict:
    key = jax.random.key(seed)
    ks = jax.random.split(key, 24)

    def nrm(k, shape, scale=1.0):
        return jax.random.normal(k, shape, jnp.float32) * scale

    win_len = [min(w, PAST_LEN) for w, _ in DIL_CONFIGS]
    return {
        'x_prompt': nrm(ks[0], (BATCH, SEQ, D_MODEL)),
        'x_sample': nrm(ks[1], (DEC_BATCH, DEC_SEQ, D_MODEL)),
        'cache_conv': nrm(ks[2], (DEPTH, DEC_BATCH, CONV_K - 1, W_A), 0.5),
        'cache_win1': nrm(ks[3], (DEPTH, DEC_BATCH, win_len[0], 2, HB, HD_B)),
        'cache_win2': nrm(ks[4], (DEPTH, DEC_BATCH, win_len[1], 2, HB, HD_B)),
        'cache_win3': nrm(ks[5], (DEPTH, DEC_BATCH, win_len[2], 2, HB, HD_B)),
        'state_ret': nrm(ks[6], (DEPTH, DEC_BATCH, HC, DK_C, DV_C), 0.5),
        'c_prompt': nrm(ks[7], (BATCH, D_MODEL)),
        'c_sample': nrm(ks[8], (DEC_BATCH, D_MODEL)),
        'w_ada': nrm(ks[9], (DEPTH, D_MODEL, 6 * D_MODEL), 0.5 * D_MODEL ** -0.5),
        'b_ada': nrm(ks[10], (DEPTH, 6 * D_MODEL), 0.02),
        'w_in': nrm(ks[11], (DEPTH, D_MODEL, D_IN), D_MODEL ** -0.5),
        'conv_w': nrm(ks[12], (DEPTH, CONV_K, W_A), CONV_K ** -0.5),
        'conv_b': nrm(ks[13], (DEPTH, W_A), 0.02),
        'conv_ln_g': 1.0 + nrm(ks[14], (DEPTH, W_A), 0.02),
        'conv_ln_b': nrm(ks[15], (DEPTH, W_A), 0.02),
        'w_o': nrm(ks[16], (DEPTH, D_MIX, D_MODEL), BETA * D_MIX ** -0.5),
        'ln1_g': 1.0 + nrm(ks[17], (DEPTH, D_MODEL), 0.02),
        'ln1_b': nrm(ks[18], (DEPTH, D_MODEL), 0.02),
        'w_up': nrm(ks[19], (DEPTH, D_MODEL, D_FF), D_MODEL ** -0.5),
        'w_down': nrm(ks[20], (DEPTH, D_FF, D_MODEL), BETA * D_FF ** -0.5),
        'ln2_g': 1.0 + nrm(ks[21], (DEPTH, D_MODEL), 0.02),
        'ln2_b': nrm(ks[22], (DEPTH, D_MODEL), 0.02),
    }


def reference(x_prompt, x_sample, cache_conv, cache_win1, cache_win2, cache_win3, state_ret,
              c_prompt, c_sample, w_ada, b_ada, w_in, conv_w, conv_b, conv_ln_g, conv_ln_b,
              w_o, ln1_g, ln1_b, w_up, w_down, ln2_g, ln2_b):
    log_gamma = retention_log_decay()
    pos_p = jnp.arange(x_prompt.shape[1], dtype=jnp.int32)
    pos_s = PAST_LEN + jnp.arange(x_sample.shape[1], dtype=jnp.int32)
    xp, xs = x_prompt, x_sample
    conv_p, conv_s, ret_p, ret_s = [], [], [], []
    win_p = [[] for _ in DIL_CONFIGS]
    win_s = [[] for _ in DIL_CONFIGS]
    for l in range(DEPTH):
        conv_l = (conv_w[l], conv_b[l], conv_ln_g[l], conv_ln_b[l])
        layer_w = (w_ada[l], b_ada[l], w_in[l], w_o[l], ln1_g[l], ln1_b[l],
                   w_up[l], w_down[l], ln2_g[l], ln2_b[l])
        bufs = (cache_win1[l], cache_win2[l], cache_win3[l])
        xp, (cp, wp, sp) = decoder_layer(
            xp, c_prompt, lambda z: mix_prompt(z, pos_p, *conv_l, log_gamma), *layer_w)
        xs, (cs, ws, ss) = decoder_layer(
            xs, c_sample, lambda z: mix_sample(z, pos_s, cache_conv[l], bufs, state_ret[l], *conv_l, log_gamma),
            *layer_w)
        conv_p.append(cp)
        conv_s.append(cs)
        ret_p.append(sp)
        ret_s.append(ss)
        for g in range(N_DIL):
            win_p[g].append(wp[g])
            win_s[g].append(ws[g])
    conv_prompt = jnp.stack(conv_p, 0)
    conv_sample = jnp.stack(conv_s, 0)
    win1_prompt = jnp.stack(win_p[0], 0)
    win1_sample = jnp.stack(win_s[0], 0)
    win2_prompt = jnp.stack(win_p[1], 0)
    win2_sample = jnp.stack(win_s[1], 0)
    win3_prompt = jnp.stack(win_p[2], 0)
    win3_sample = jnp.stack(win_s[2], 0)
    ret_prompt = jnp.stack(ret_p, 0)
    ret_sample = jnp.stack(ret_s, 0)
    return (xp, xs, conv_prompt, conv_sample, win1_prompt, win1_sample, win2_prompt, win2_sample,
            win3_prompt, win3_sample, ret_prompt, ret_sample)
```

```python
import functools

import numpy as np
import jax
import jax.numpy as jnp
from jax import lax
from jax.experimental import pallas as pl
from jax.experimental.pallas import tpu as pltpu

F32 = jnp.float32
BF16 = jnp.bfloat16

D_MODEL = 2048
BATCH = 4
SEQ = 2048
DEPTH = 4
DEC_BATCH = 8
DEC_SEQ = 4
PAST_LEN = 16384
W_A = D_MODEL // 4
CONV_K = 31
HD_B = 128
W_B = D_MODEL // 4
HB = W_B // HD_B
DIL_CONFIGS = ((128, 1), (512, 4), (2048, 16))
N_DIL = len(DIL_CONFIGS)
Q_BLOCK = 128
W_C = D_MODEL // 2
HC = 4
DV_C = W_C // HC
DK_C = DV_C // 2
RET_CHUNK = 128
D_IN = 2 * W_A + 3 * N_DIL * W_B + 2 * HC * DK_C + 2 * W_C
D_FF = 4 * D_MODEL
ROPE_THETA = 10000.0
ALPHA = (2 * DEPTH) ** 0.25
LN_EPS = 1e-5
NEG_INF = -1e30

OFF_A = 0
OFF_GA = W_A
OFF_QB = 2 * W_A
OFF_KB = OFF_QB + N_DIL * W_B
OFF_VB = OFF_KB + N_DIL * W_B
OFF_QC = OFF_VB + N_DIL * W_B
OFF_KC = OFF_QC + HC * DK_C
OFF_VC = OFF_KC + HC * DK_C
OFF_GC = OFF_VC + W_C

SUBLANES = 8
LANES = 128
SAMPLE_ROWS = SUBLANES
CONV_HALO = 32
MIB = 1024 * 1024


def _params(n_axes, vmem_mib):
    return pltpu.CompilerParams(dimension_semantics=("arbitrary",) * n_axes,
                                vmem_limit_bytes=vmem_mib * MIB)


def _rope_tables(pos):
    half = HD_B // 2
    inv = ROPE_THETA ** (-np.arange(half, dtype=np.float64) / half)
    ang = np.asarray(pos, np.float64)[:, None] * inv[None, :]
    cos, sin = np.cos(ang), np.sin(ang)
    return (np.concatenate([cos, cos], -1).astype(np.float32),
            np.concatenate([-sin, sin], -1).astype(np.float32))


def _rope(x, cos, sin):
    return x * cos + pltpu.roll(x, HD_B // 2, 1) * sin


def _retention_tables(chunk, n_valid):
    lg = np.log1p(-np.exp2(-5.0 - np.arange(HC, dtype=np.float64)))
    n = np.arange(chunk, dtype=np.float64)
    diff = n[:, None] - n[None, :]
    decay = np.where(diff >= 0, np.exp(lg[:, None, None] * np.maximum(diff, 0.0)), 0.0)
    inner = np.exp(lg[:, None] * (n[None, :] + 1.0))
    kdec = np.where(n[None, :] < n_valid, np.exp(lg[:, None] * (n_valid - 1.0 - n[None, :])), 0.0)
    cdec = np.exp(lg * n_valid)
    inner_b = np.broadcast_to(inner[:, :, None], (HC, chunk, DV_C))
    kdec_b = np.broadcast_to(kdec[:, :, None], (HC, chunk, DK_C))
    cdec_b = np.broadcast_to(cdec[:, None, None], (HC, 1, DV_C))
    f = lambda a: jnp.asarray(np.ascontiguousarray(a), F32)
    return f(decay), f(inner_b), f(kdec_b), f(cdec_b)


def _layer_norm_rows(r, g, b):
    mu = jnp.mean(r, axis=-1, keepdims=True)
    xc = r - mu
    var = jnp.mean(xc * xc, axis=-1, keepdims=True)
    y = xc * lax.rsqrt(var + LN_EPS)
    if g is not None:
        y = y * g + b
    return y


def _ada_kernel(c_ref, w_ref, b_ref, o_ref):
    c = c_ref[...]
    s = (c * jax.nn.sigmoid(c)).astype(BF16)
    o_ref[...] = jnp.dot(s, w_ref[...].astype(BF16), preferred_element_type=F32) + b_ref[...]


def _ada(c_all, w_ada, b_ada):
    rows = c_all.shape[0]
    tn = 1024
    return pl.pallas_call(
        _ada_kernel,
        grid=(DEPTH, 6 * D_MODEL // tn),
        in_specs=[
            pl.BlockSpec((rows, D_MODEL), lambda l, j: (0, 0)),
            pl.BlockSpec((None, D_MODEL, tn), lambda l, j: (l, 0, j)),
            pl.BlockSpec((None, 1, tn), lambda l, j: (l, 0, j)),
        ],
        out_specs=pl.BlockSpec((None, rows, tn), lambda l, j: (l, 0, j)),
        out_shape=jax.ShapeDtypeStruct((DEPTH, rows, 6 * D_MODEL), F32),
        compiler_params=_params(2, 40),
        name="ada_mod",
    )(c_all, w_ada, b_ada.reshape(DEPTH, 1, 6 * D_MODEL))


def _in_kernel(x_ref, sc_ref, sh_ref, w_ref, o_ref, h_ref):
    @pl.when(pl.program_id(2) == 0)
    def _():
        h_ref[...] = (x_ref[...] * (1.0 + sc_ref[...]) + sh_ref[...]).astype(BF16)

    o_ref[...] = jnp.dot(h_ref[...], w_ref[...].astype(BF16), preferred_element_type=F32)


def _in_proj(x, sc, sh, w_in, layer, tm):
    nb, rows, _ = x.shape
    mod_rows = sc.shape[1]
    tn = 512
    return pl.pallas_call(
        _in_kernel,
        grid=(nb, rows // tm, D_IN // tn),
        in_specs=[
            pl.BlockSpec((None, tm, D_MODEL), lambda b, t, j: (b, t, 0)),
            pl.BlockSpec((None, mod_rows, D_MODEL), lambda b, t, j: (b, 0, 0)),
            pl.BlockSpec((None, mod_rows, D_MODEL), lambda b, t, j: (b, 0, 0)),
            pl.BlockSpec((None, D_MODEL, tn), lambda b, t, j: (layer, 0, j)),
        ],
        out_specs=pl.BlockSpec((None, tm, tn), lambda b, t, j: (b, t, j)),
        out_shape=jax.ShapeDtypeStruct((nb, rows, D_IN), F32),
        scratch_shapes=[pltpu.VMEM((tm, D_MODEL), BF16)],
        compiler_params=_params(3, 48),
        name="in_proj",
    )(x, sc, sh, w_in)


def _conv_kernel(a_ref, ga_ref, init_ref, w_ref, cb_ref, g_ref, b_ref, y_ref, st_ref, ubuf,
                 *, tq, rb, n_valid_last, n_tiles):
    t = pl.program_id(1)

    @pl.when(t == 0)
    def _():
        ubuf[0:CONV_HALO, :] = init_ref[...]

    ga = ga_ref[...]
    ubuf[CONV_HALO:CONV_HALO + tq, :] = a_ref[...] * jax.nn.sigmoid(ga)
    first = CONV_HALO - (CONV_K - 1)
    for blk in range(tq // rb):
        r0 = blk * rb
        acc = jnp.broadcast_to(cb_ref[...], (rb, W_A))
        for k in range(CONV_K):
            acc = acc + w_ref[pl.ds(k, 1), :] * ubuf[r0 + first + k:r0 + first + k + rb, :]
        yv = _layer_norm_rows(acc, g_ref[...], b_ref[...])
        y_ref[r0:r0 + rb, :] = (yv * jax.nn.sigmoid(yv)).astype(BF16)

    @pl.when(t == n_tiles - 1)
    def _():
        st_ref[...] = ubuf[n_valid_last:n_valid_last + CONV_HALO, :]

    halo = ubuf[tq:tq + CONV_HALO, :]
    ubuf[0:CONV_HALO, :] = halo


def _conv_mixer(z, init, conv_w, conv_b, conv_g, conv_beta, layer, tq, n_valid_last):
    nb, rows, _ = z.shape
    n_tiles = rows // tq
    rb = min(tq, 32)
    vec = pl.BlockSpec((None, 1, W_A), lambda b, t: (layer, 0, 0))
    kern = functools.partial(_conv_kernel, tq=tq, rb=rb, n_valid_last=n_valid_last, n_tiles=n_tiles)
    return pl.pallas_call(
        kern,
        grid=(nb, n_tiles),
        in_specs=[
            pl.BlockSpec((None, tq, W_A), lambda b, t: (b, t, OFF_A // W_A)),
            pl.BlockSpec((None, tq, W_A), lambda b, t: (b, t, OFF_GA // W_A)),
            pl.BlockSpec((None, CONV_HALO, W_A), lambda b, t: (b, 0, 0)),
            pl.BlockSpec((None, CONV_K, W_A), lambda b, t: (layer, 0, 0)),
            vec, vec, vec,
        ],
        out_specs=[
            pl.BlockSpec((None, tq, W_A), lambda b, t: (b, t, 0)),
            pl.BlockSpec((None, CONV_HALO, W_A), lambda b, t: (b, 0, 0)),
        ],
        out_shape=[
            jax.ShapeDtypeStruct((nb, rows, W_A), BF16),
            jax.ShapeDtypeStruct((nb, CONV_HALO, W_A), F32),
        ],
        scratch_shapes=[pltpu.VMEM((CONV_HALO + tq, W_A), F32)],
        compiler_params=_params(2, 32),
        name="conv_mixer",
    )(z, z, init, conv_w, conv_b.reshape(DEPTH, 1, W_A), conv_g.reshape(DEPTH, 1, W_A),
      conv_beta.reshape(DEPTH, 1, W_A))


ATT_SCALE = HD_B ** -0.5


def _attn_prompt_kernel(q1, k1, v1, q2, k2, v2, q3, k3, v3, cos_ref, sin_ref,
                        yb_ref, kr1_ref, kr2_ref, kr3_ref, kr_s, o_s, l_s):
    qs, ks, vs = (q1, q2, q3), (k1, k2, k3), (v1, v2, v3)
    kr_outs = (kr1_ref, kr2_ref, kr3_ref)
    rows = 256

    for g in range(N_DIL):
        def rope_k(i, carry, g=g):
            s = pl.multiple_of(i * rows, rows)
            kr_s[g, pl.ds(s, rows), :] = _rope(ks[g][pl.ds(s, rows), :], cos_ref[pl.ds(s, rows), :],
                                               sin_ref[pl.ds(s, rows), :])
            return carry

        lax.fori_loop(0, SEQ // rows, rope_k, 0)
        win = min(DIL_CONFIGS[g][0], SEQ)
        kr_outs[g][...] = kr_s[g, SEQ - win:SEQ, :]

    qi = lax.broadcasted_iota(jnp.int32, (Q_BLOCK, Q_BLOCK), 0)
    kj = lax.broadcasted_iota(jnp.int32, (Q_BLOCK, Q_BLOCK), 1)

    for g, (window, dil) in enumerate(DIL_CONFIGS):
        nblk = (SEQ // dil) // Q_BLOCK
        span = Q_BLOCK * dil

        def rows_of(start, dil=dil):
            if dil == 1:
                return pl.ds(pl.multiple_of(start, Q_BLOCK), Q_BLOCK)
            return pl.ds(start, Q_BLOCK, stride=dil)

        def block(it, carry, g=g, nblk=nblk, span=span, rows_of=rows_of):
            r = it // nblk
            n = it % nblk
            start = r + n * span
            cur = rows_of(start)
            q = _rope(qs[g][cur, :], cos_ref[cur, :], sin_ref[cur, :]).astype(BF16)
            k_cur = kr_s[g, cur, :].astype(BF16)
            v_cur = vs[g][cur, :].astype(BF16)
            s_cur = lax.dot_general(q, k_cur, (((1,), (1,)), ((), ())),
                                    preferred_element_type=F32) * ATT_SCALE
            s_cur = jnp.where(kj <= qi, s_cur, NEG_INF)
            m = jnp.max(s_cur, axis=-1, keepdims=True)
            if nblk > 1:
                prev = rows_of(r + jnp.maximum(n - 1, 0) * span)
                k_prev = kr_s[g, prev, :].astype(BF16)
                v_prev = vs[g][prev, :].astype(BF16)
                s_prev = lax.dot_general(q, k_prev, (((1,), (1,)), ((), ())),
                                         preferred_element_type=F32) * ATT_SCALE
                s_prev = jnp.where((kj >= qi) & (n > 0), s_prev, NEG_INF)
                m = jnp.maximum(m, jnp.max(s_prev, axis=-1, keepdims=True))
                p_prev = jnp.exp(s_prev - m)
            p_cur = jnp.exp(s_cur - m)
            den = jnp.sum(p_cur, axis=-1, keepdims=True)
            o = jnp.dot(p_cur.astype(BF16), v_cur, preferred_element_type=F32)
            if nblk > 1:
                den = den + jnp.sum(p_prev, axis=-1, keepdims=True)
                o = o + jnp.dot(p_prev.astype(BF16), v_prev, preferred_element_type=F32)
            o_s[g, cur, :] = o / den
            l_s[g, cur, :] = jnp.broadcast_to(m + jnp.log(den), (Q_BLOCK, HD_B))
            return carry

        lax.fori_loop(0, dil * nblk, block, 0)

    def merge(i, carry):
        s = pl.multiple_of(i * rows, rows)
        sl = pl.ds(s, rows)
        l0, l1, l2 = l_s[0, sl, :], l_s[1, sl, :], l_s[2, sl, :]
        m = jnp.maximum(jnp.maximum(l0, l1), l2)
        w0, w1, w2 = jnp.exp(l0 - m), jnp.exp(l1 - m), jnp.exp(l2 - m)
        o = (w0 * o_s[0, sl, :] + w1 * o_s[1, sl, :] + w2 * o_s[2, sl, :]) / (w0 + w1 + w2)
        yb_ref[sl, :] = o.astype(BF16)
        return carry

    lax.fori_loop(0, SEQ // rows, merge, 0)


def _attn_prompt(z, cos, sin):
    nb = z.shape[0]
    head_cols = lambda off, g: (lambda b, h: (b, 0, off // HD_B + g * HB + h))
    in_specs = []
    for g in range(N_DIL):
        for off in (OFF_QB, OFF_KB, OFF_VB):
            in_specs.append(pl.BlockSpec((None, SEQ, HD_B), head_cols(off, g)))
    table = pl.BlockSpec((SEQ, HD_B), lambda b, h: (0, 0))
    wins = [min(w, SEQ) for w, _ in DIL_CONFIGS]
    return pl.pallas_call(
        _attn_prompt_kernel,
        grid=(nb, HB),
        in_specs=in_specs + [table, table],
        out_specs=[pl.BlockSpec((None, SEQ, HD_B), lambda b, h: (b, 0, h))]
        + [pl.BlockSpec((None, w, HD_B), lambda b, h: (b, 0, h)) for w in wins],
        out_shape=[jax.ShapeDtypeStruct((nb, SEQ, W_B), BF16)]
        + [jax.ShapeDtypeStruct((nb, w, W_B), F32) for w in wins],
        scratch_shapes=[pltpu.VMEM((N_DIL, SEQ, HD_B), F32)] * 3,
        compiler_params=_params(2, 48),
        name="attn_prompt",
    )(*([z] * 9), cos, sin)


def _attn_sample_kernel(z_ref, c1_ref, c2_ref, c3_ref, cos_ref, sin_ref, yb_ref, kn_ref, o_acc):
    caches = (c1_ref, c2_ref, c3_ref)
    cos, sin = cos_ref[...], sin_ref[...]
    tok = lax.broadcasted_iota(jnp.int32, (SAMPLE_ROWS, 1), 0)
    row = lax.broadcasted_iota(jnp.int32, (Q_BLOCK, 1), 0)
    o_acc[...] = jnp.zeros((SAMPLE_ROWS, W_B), F32)

    for h in range(HB):
        outs, lses = [], []
        for g, (window, dil) in enumerate(DIL_CONFIGS):
            col = (g * HB + h) * HD_B
            q = _rope(z_ref[:, OFF_QB + col:OFF_QB + col + HD_B], cos, sin)
            k_new = _rope(z_ref[:, OFF_KB + col:OFF_KB + col + HD_B], cos, sin)
            v_new = z_ref[:, OFF_VB + col:OFF_VB + col + HD_B]
            kn_ref[:, col:col + HD_B] = k_new
            length = min(window, PAST_LEN)
            nw = window // dil
            o_rows, l_rows = [], []
            for t in range(DEC_SEQ):
                q_t = q[t:t + 1, :]
                per_pos = 2 * HB
                if dil == 1:
                    base = length - nw
                    ok_c = row >= t
                    ok_n = tok <= t
                else:
                    base = length - nw * dil + t
                    ok_c = None
                    ok_n = tok == t
                k_c = caches[g][pl.ds(base * per_pos + h, nw, stride=per_pos * dil), :]
                v_c = caches[g][pl.ds(base * per_pos + HB + h, nw, stride=per_pos * dil), :]
                s_c = jnp.sum(k_c * q_t, axis=-1, keepdims=True) * ATT_SCALE
                if ok_c is not None:
                    s_c = jnp.where(ok_c, s_c, NEG_INF)
                s_n = jnp.sum(k_new * q_t, axis=-1, keepdims=True) * ATT_SCALE
                s_n = jnp.where(ok_n, s_n, NEG_INF)
                m = jnp.maximum(jnp.max(s_c, axis=0, keepdims=True), jnp.max(s_n, axis=0, keepdims=True))
                p_c = jnp.exp(s_c - m)
                p_n = jnp.exp(s_n - m)
                den = jnp.sum(p_c, axis=0, keepdims=True) + jnp.sum(p_n, axis=0, keepdims=True)
                o = (jnp.sum(p_c * v_c, axis=0, keepdims=True)
                     + jnp.sum(p_n * v_new, axis=0, keepdims=True)) / den
                o_rows.append(o)
                l_rows.append(m + jnp.log(den))
            outs.append(o_rows)
            lses.append(l_rows)
        for t in range(DEC_SEQ):
            l0, l1, l2 = lses[0][t], lses[1][t], lses[2][t]
            m = jnp.maximum(jnp.maximum(l0, l1), l2)
            w0, w1, w2 = jnp.exp(l0 - m), jnp.exp(l1 - m), jnp.exp(l2 - m)
            o = (w0 * outs[0][t] + w1 * outs[1][t] + w2 * outs[2][t]) / (w0 + w1 + w2)
            o_acc[t:t + 1, h * HD_B:(h + 1) * HD_B] = o
    yb_ref[...] = o_acc[...].astype(BF16)


def _attn_sample(z, caches, cos, sin, layer):
    nb = z.shape[0]
    cache_specs = [pl.BlockSpec((None, None, c.shape[2], HD_B), lambda b: (layer, b, 0, 0)) for c in caches]
    table = pl.BlockSpec((SAMPLE_ROWS, HD_B), lambda b: (0, 0))
    return pl.pallas_call(
        _attn_sample_kernel,
        grid=(nb,),
        in_specs=[pl.BlockSpec((None, SAMPLE_ROWS, D_IN), lambda b: (b, 0, 0))] + cache_specs + [table, table],
        out_specs=[
            pl.BlockSpec((None, SAMPLE_ROWS, W_B), lambda b: (b, 0, 0)),
            pl.BlockSpec((None, SAMPLE_ROWS, N_DIL * W_B), lambda b: (b, 0, 0)),
        ],
        out_shape=[
            jax.ShapeDtypeStruct((nb, SAMPLE_ROWS, W_B), BF16),
            jax.ShapeDtypeStruct((nb, SAMPLE_ROWS, N_DIL * W_B), F32),
        ],
        scratch_shapes=[pltpu.VMEM((SAMPLE_ROWS, W_B), F32)],
        compiler_params=_params(1, 40),
        name="attn_sample",
    )(z, *caches, cos, sin)


def _ret_kernel(q_ref, k_ref, v_ref, g_ref, cos_ref, sin_ref, dec_ref, inner_ref, kdec_ref, cdec_ref,
                s0_ref, y_ref, s_out_ref, state, *stage, rows):
    state[...] = s0_ref[...]
    chunk = RET_CHUNK

    def step(q, k, v, gate):
        qb = q.astype(BF16)
        vb = v.astype(BF16)
        sc = lax.dot_general(qb, k.astype(BF16), (((1,), (1,)), ((), ())),
                             preferred_element_type=F32) * dec_ref[...]
        o = jnp.dot(sc.astype(BF16), vb, preferred_element_type=F32)
        o = o + jnp.dot(qb, state[...].astype(BF16), preferred_element_type=F32) * inner_ref[...]
        kk = (k * kdec_ref[...]).astype(BF16)
        state[...] = state[...] * cdec_ref[...] + lax.dot_general(
            kk, vb, (((0,), (0,)), ((), ())), preferred_element_type=F32)
        gn = _layer_norm_rows(o, None, None)
        return (gate * jax.nn.sigmoid(gate)) * gn

    if rows % chunk == 0:
        def body(c, carry):
            sl = pl.ds(pl.multiple_of(c * chunk, chunk), chunk)
            cos, sin = cos_ref[sl, :], sin_ref[sl, :]
            q = _rope(q_ref[sl, :], cos, sin)
            k = _rope(k_ref[sl, :], cos, sin) * (DK_C ** -0.5)
            y_ref[sl, :] = step(q, k, v_ref[sl, :], g_ref[sl, :]).astype(BF16)
            return carry

        lax.fori_loop(0, rows // chunk, body, 0)
    else:
        qbuf, kbuf, vbuf, gbuf = stage
        qbuf[...] = jnp.zeros((chunk, DK_C), F32)
        kbuf[...] = jnp.zeros((chunk, DK_C), F32)
        vbuf[...] = jnp.zeros((chunk, DV_C), F32)
        gbuf[...] = jnp.zeros((chunk, DV_C), F32)
        cos, sin = cos_ref[...], sin_ref[...]
        qbuf[0:rows, :] = _rope(q_ref[...], cos, sin)
        kbuf[0:rows, :] = _rope(k_ref[...], cos, sin) * (DK_C ** -0.5)
        vbuf[0:rows, :] = v_ref[...]
        gbuf[0:rows, :] = g_ref[...]
        y = step(qbuf[...], kbuf[...], vbuf[...], gbuf[...])
        y_ref[...] = y[0:rows, :].astype(BF16)

    s_out_ref[...] = state[...]


def _retention(z, s0, cos, sin, tables, layer_state=None):
    nb, rows, _ = z.shape
    decay, inner_b, kdec_b, cdec_b = tables
    if layer_state is None:
        s0_spec = pl.BlockSpec((None, None, DK_C, DV_C), lambda b, h: (b, h, 0, 0))
    else:
        s0_spec = pl.BlockSpec((None, None, None, DK_C, DV_C), lambda b, h: (layer_state, b, h, 0, 0))
    staged = rows % RET_CHUNK != 0
    scratch = [pltpu.VMEM((DK_C, DV_C), F32)]
    if staged:
        scratch += [pltpu.VMEM((RET_CHUNK, DK_C), F32), pltpu.VMEM((RET_CHUNK, DK_C), F32),
                    pltpu.VMEM((RET_CHUNK, DV_C), F32), pltpu.VMEM((RET_CHUNK, DV_C), F32)]
    table_rows = cos.shape[0]
    return pl.pallas_call(
        functools.partial(_ret_kernel, rows=rows),
        grid=(nb, HC),
        in_specs=[
            pl.BlockSpec((None, rows, DK_C), lambda b, h: (b, 0, OFF_QC // DK_C + h)),
            pl.BlockSpec((None, rows, DK_C), lambda b, h: (b, 0, OFF_KC // DK_C + h)),
            pl.BlockSpec((None, rows, DV_C), lambda b, h: (b, 0, OFF_VC // DV_C + h)),
            pl.BlockSpec((None, rows, DV_C), lambda b, h: (b, 0, OFF_GC // DV_C + h)),
            pl.BlockSpec((table_rows, DK_C), lambda b, h: (0, 0)),
            pl.BlockSpec((table_rows, DK_C), lambda b, h: (0, 0)),
            pl.BlockSpec((None, RET_CHUNK, RET_CHUNK), lambda b, h: (h, 0, 0)),
            pl.BlockSpec((None, RET_CHUNK, DV_C), lambda b, h: (h, 0, 0)),
            pl.BlockSpec((None, RET_CHUNK, DK_C), lambda b, h: (h, 0, 0)),
            pl.BlockSpec((None, 1, DV_C), lambda b, h: (h, 0, 0)),
            s0_spec,
        ],
        out_specs=[
            pl.BlockSpec((None, rows, DV_C), lambda b, h: (b, 0, h)),
            pl.BlockSpec((None, None, DK_C, DV_C), lambda b, h: (b, h, 0, 0)),
        ],
        out_shape=[
            jax.ShapeDtypeStruct((nb, rows, W_C), BF16),
            jax.ShapeDtypeStruct((nb, HC, DK_C, DV_C), F32),
        ],
        scratch_shapes=scratch,
        compiler_params=_params(2, 40),
        name="retention",
    )(z, z, z, z, cos, sin, decay, inner_b, kdec_b, cdec_b, s0)


def _out_kernel(ya_ref, yb_ref, yc_ref, w_ref, x_ref, g1_ref, sc_ref, sh_ref, lg_ref, lb_ref,
                xo_ref, ho_ref, wb_ref):
    @pl.when((pl.program_id(0) == 0) & (pl.program_id(1) == 0))
    def _():
        wb_ref[...] = w_ref[...].astype(BF16)

    acc = jnp.dot(ya_ref[...], wb_ref[0:W_A, :], preferred_element_type=F32)
    acc = acc + jnp.dot(yb_ref[...], wb_ref[W_A:W_A + W_B, :], preferred_element_type=F32)
    acc = acc + jnp.dot(yc_ref[...], wb_ref[W_A + W_B:, :], preferred_element_type=F32)
    r = ALPHA * x_ref[...] + g1_ref[...] * acc
    x1 = _layer_norm_rows(r, lg_ref[...], lb_ref[...])
    xo_ref[...] = x1
    ho_ref[...] = (x1 * (1.0 + sc_ref[...]) + sh_ref[...]).astype(BF16)


def _out_proj(ya, yb, yc, w_o, x, g1, sc2, sh2, ln_g, ln_b, layer, tm):
    nb, rows, _ = x.shape
    mod_rows = g1.shape[1]
    tile = lambda width: pl.BlockSpec((None, tm, width), lambda b, t: (b, t, 0))
    mod = pl.BlockSpec((None, mod_rows, D_MODEL), lambda b, t: (b, 0, 0))
    vec = pl.BlockSpec((None, 1, D_MODEL), lambda b, t: (layer, 0, 0))
    return pl.pallas_call(
        _out_kernel,
        grid=(nb, rows // tm),
        in_specs=[
            tile(W_A), tile(W_B), tile(W_C),
            pl.BlockSpec((None, D_MODEL, D_MODEL), lambda b, t: (layer, 0, 0), pipeline_mode=pl.Buffered(1)),
            tile(D_MODEL), mod, mod, mod, vec, vec,
        ],
        out_specs=[tile(D_MODEL), tile(D_MODEL)],
        out_shape=[
            jax.ShapeDtypeStruct((nb, rows, D_MODEL), F32),
            jax.ShapeDtypeStruct((nb, rows, D_MODEL), BF16),
        ],
        scratch_shapes=[pltpu.VMEM((D_MODEL, D_MODEL), BF16)],
        compiler_params=_params(2, 56),
        name="out_proj",
    )(ya, yb, yc, w_o, x, g1, sc2, sh2, ln_g.reshape(DEPTH, 1, D_MODEL), ln_b.reshape(DEPTH, 1, D_MODEL))


def _ffn_kernel(h_ref, wu_ref, wd_ref, x_ref, g2_ref, lg_ref, lb_ref, o_ref, *, n_f):
    f = pl.program_id(2)
    u = jnp.dot(h_ref[...], wu_ref[...].astype(BF16), preferred_element_type=F32)
    a = jnp.square(jnp.maximum(u, 0.0)).astype(BF16)
    half = D_MODEL // 2
    for c in range(2):
        cols = slice(c * half, (c + 1) * half)
        part = jnp.dot(a, wd_ref[:, cols].astype(BF16), preferred_element_type=F32)

        @pl.when(f == 0)
        def _(part=part, cols=cols):
            o_ref[:, cols] = part

        @pl.when(f > 0)
        def _(part=part, cols=cols):
            o_ref[:, cols] += part

    @pl.when(f == n_f - 1)
    def _():
        r = ALPHA * x_ref[...] + g2_ref[...] * o_ref[...]
        o_ref[...] = _layer_norm_rows(r, lg_ref[...], lb_ref[...])


def _ffn(h, w_up, w_down, x, g2, ln_g, ln_b, layer, tm):
    nb, rows, _ = x.shape
    mod_rows = g2.shape[1]
    tf = 512
    n_f = D_FF // tf
    once = pl.Buffered(1)
    vec = pl.BlockSpec((None, 1, D_MODEL), lambda b, t, f: (layer, 0, 0))
    return pl.pallas_call(
        functools.partial(_ffn_kernel, n_f=n_f),
        grid=(nb, rows // tm, n_f),
        in_specs=[
            pl.BlockSpec((None, tm, D_MODEL), lambda b, t, f: (b, t, 0), pipeline_mode=once),
            pl.BlockSpec((None, D_MODEL, tf), lambda b, t, f: (layer, 0, f)),
            pl.BlockSpec((None, tf, D_MODEL), lambda b, t, f: (layer, f, 0)),
            pl.BlockSpec((None, tm, D_MODEL), lambda b, t, f: (b, t, 0), pipeline_mode=once),
            pl.BlockSpec((None, mod_rows, D_MODEL), lambda b, t, f: (b, 0, 0)),
            vec, vec,
        ],
        out_specs=pl.BlockSpec((None, tm, D_MODEL), lambda b, t, f: (b, t, 0), pipeline_mode=once),
        out_shape=jax.ShapeDtypeStruct((nb, rows, D_MODEL), F32),
        compiler_params=_params(3, 56),
        name="ffn",
    )(h, w_up, w_down, x, g2, ln_g.reshape(DEPTH, 1, D_MODEL), ln_b.reshape(DEPTH, 1, D_MODEL))


def kernel(x_prompt, x_sample, cache_conv, cache_win1, cache_win2, cache_win3, state_ret, c_prompt, c_sample,
           w_ada, b_ada, w_in, conv_w, conv_b, conv_ln_g, conv_ln_b, w_o, ln1_g, ln1_b, w_up, w_down,
           ln2_g, ln2_b):
    assert x_prompt.shape == (BATCH, SEQ, D_MODEL) and x_sample.shape == (DEC_BATCH, DEC_SEQ, D_MODEL)
    pad_tok = SAMPLE_ROWS - DEC_SEQ
    n_s = DEC_BATCH * SAMPLE_ROWS

    cos_p, sin_p = (jnp.asarray(t) for t in _rope_tables(np.arange(SEQ)))
    cos_s, sin_s = (jnp.asarray(t) for t in _rope_tables(PAST_LEN + np.arange(SAMPLE_ROWS)))
    tab_p = _retention_tables(RET_CHUNK, RET_CHUNK)
    tab_s = _retention_tables(RET_CHUNK, DEC_SEQ)

    c_all = jnp.concatenate([c_prompt, c_sample, jnp.zeros((4, D_MODEL), F32)], axis=0)
    mod = _ada(c_all, w_ada, b_ada)

    caches = tuple(c.reshape(DEPTH, DEC_BATCH, c.shape[2] * 2 * HB, HD_B)
                   for c in (cache_win1, cache_win2, cache_win3))
    conv_hist_s = jnp.pad(cache_conv, ((0, 0), (0, 0), (CONV_HALO - (CONV_K - 1), 0), (0, 0)))
    conv_hist_p = jnp.zeros((BATCH, CONV_HALO, W_A), F32)
    ret_zero = jnp.zeros((BATCH, HC, DK_C, DV_C), F32)

    xp = x_prompt
    xs = jnp.pad(x_sample, ((0, 0), (0, pad_tok), (0, 0))).reshape(1, n_s, D_MODEL)

    conv_p, conv_s, ret_p, ret_s = [], [], [], []
    win_p = [[] for _ in DIL_CONFIGS]
    new_s = [[] for _ in DIL_CONFIGS]
    for l in range(DEPTH):
        mods_p = [mod[l, :BATCH, i * D_MODEL:(i + 1) * D_MODEL][:, None, :] for i in range(6)]
        mods_s = [jnp.repeat(mod[l, BATCH:BATCH + DEC_BATCH, i * D_MODEL:(i + 1) * D_MODEL], SAMPLE_ROWS,
                             axis=0)[None] for i in range(6)]

        sh1, sc1, g1, sh2, sc2, g2 = mods_p
        z = _in_proj(xp, sc1, sh1, w_in, l, tm=1024)
        ya, cst = _conv_mixer(z, conv_hist_p, conv_w, conv_b, conv_ln_g, conv_ln_b, l, tq=256, n_valid_last=256)
        yb, kr1, kr2, kr3 = _attn_prompt(z, cos_p, sin_p)
        yc, sp = _retention(z, ret_zero, cos_p, sin_p, tab_p)
        x1, h2 = _out_proj(ya, yb, yc, w_o, xp, g1, sc2, sh2, ln1_g, ln1_b, l, tm=256)
        xp = _ffn(h2, w_up, w_down, x1, g2, ln2_g, ln2_b, l, tm=1024)
        conv_p.append(cst[:, CONV_HALO - (CONV_K - 1):])
        ret_p.append(sp)
        for g, kr in enumerate((kr1, kr2, kr3)):
            n_win = kr.shape[1]
            v = z[:, SEQ - n_win:, OFF_VB + g * W_B:OFF_VB + (g + 1) * W_B]
            win_p[g].append(jnp.stack([kr.reshape(BATCH, n_win, HB, HD_B), v.reshape(BATCH, n_win, HB, HD_B)],
                                      axis=2))

        sh1, sc1, g1, sh2, sc2, g2 = mods_s
        zs = _in_proj(xs, sc1, sh1, w_in, l, tm=n_s)
        zs3 = zs.reshape(DEC_BATCH, SAMPLE_ROWS, D_IN)
        ya, cst = _conv_mixer(zs3, conv_hist_s[l], conv_w, conv_b, conv_ln_g, conv_ln_b, l,
                              tq=SAMPLE_ROWS, n_valid_last=DEC_SEQ)
        yb, k_new = _attn_sample(zs3, caches, cos_s, sin_s, l)
        yc, ss = _retention(zs3, state_ret, cos_s, sin_s, tab_s, layer_state=l)
        flat = lambda a: a.reshape(1, n_s, a.shape[-1])
        x1, h2 = _out_proj(flat(ya), flat(yb), flat(yc), w_o, xs, g1, sc2, sh2, ln1_g, ln1_b, l, tm=n_s)
        xs = _ffn(h2, w_up, w_down, x1, g2, ln2_g, ln2_b, l, tm=n_s)
        conv_s.append(cst[:, CONV_HALO - (CONV_K - 1):])
        ret_s.append(ss)
        for g in range(N_DIL):
            kn = k_new[:, :DEC_SEQ, g * W_B:(g + 1) * W_B].reshape(DEC_BATCH, DEC_SEQ, HB, HD_B)
            vn = zs3[:, :DEC_SEQ, OFF_VB + g * W_B:OFF_VB + (g + 1) * W_B].reshape(DEC_BATCH, DEC_SEQ, HB, HD_B)
            new_s[g].append(jnp.stack([kn, vn], axis=2))

    y_sample = xs.reshape(DEC_BATCH, SAMPLE_ROWS, D_MODEL)[:, :DEC_SEQ]
    wins_s = [jnp.concatenate([c[:, :, DEC_SEQ:], jnp.stack(new_s[g], 0)], axis=2)
              for g, c in enumerate((cache_win1, cache_win2, cache_win3))]
    return (xp, y_sample, jnp.stack(conv_p, 0), jnp.stack(conv_s, 0),
            jnp.stack(win_p[0], 0), wins_s[0], jnp.stack(win_p[1], 0), wins_s[1],
            jnp.stack(win_p[2], 0), wins_s[2], jnp.stack(ret_p, 0), jnp.stack(ret_s, 0))
```

```python
import functools

import numpy as np
import jax
import jax.numpy as jnp
from jax import lax
from jax.experimental import pallas as pl
from jax.experimental.pallas import tpu as pltpu

F32 = jnp.float32
BF16 = jnp.bfloat16

D_MODEL = 2048
BATCH = 4
SEQ = 2048
DEPTH = 4
DEC_BATCH = 8
DEC_SEQ = 4
PAST_LEN = 16384
W_A = D_MODEL // 4
CONV_K = 31
HD_B = 128
W_B = D_MODEL // 4
HB = W_B // HD_B
DIL_CONFIGS = ((128, 1), (512, 4), (2048, 16))
N_DIL = len(DIL_CONFIGS)
Q_BLOCK = 128
W_C = D_MODEL // 2
HC = 4
DV_C = W_C // HC
DK_C = DV_C // 2
RET_CHUNK = 128
D_IN = 2 * W_A + 3 * N_DIL * W_B + 2 * HC * DK_C + 2 * W_C
D_FF = 4 * D_MODEL
ROPE_THETA = 10000.0
ALPHA = (2 * DEPTH) ** 0.25
LN_EPS = 1e-5
NEG_INF = -1e30

OFF_A = 0
OFF_GA = W_A
OFF_QB = 2 * W_A
OFF_KB = OFF_QB + N_DIL * W_B
OFF_VB = OFF_KB + N_DIL * W_B
OFF_QC = OFF_VB + N_DIL * W_B
OFF_KC = OFF_QC + HC * DK_C
OFF_VC = OFF_KC + HC * DK_C
OFF_GC = OFF_VC + W_C

SUBLANES = 8
LANES = 128
SAMPLE_ROWS = SUBLANES
CONV_HALO = 32
MIB = 1024 * 1024


def _params(n_axes, vmem_mib):
    return pltpu.CompilerParams(dimension_semantics=("arbitrary",) * n_axes,
                                vmem_limit_bytes=vmem_mib * MIB)


def _rope_tables(pos):
    half = HD_B // 2
    inv = ROPE_THETA ** (-np.arange(half, dtype=np.float64) / half)
    ang = np.asarray(pos, np.float64)[:, None] * inv[None, :]
    cos, sin = np.cos(ang), np.sin(ang)
    return (np.concatenate([cos, cos], -1).astype(np.float32),
            np.concatenate([-sin, sin], -1).astype(np.float32))


def _rope(x, cos, sin):
    return x * cos + pltpu.roll(x, HD_B // 2, 1) * sin


def _retention_tables(chunk, n_valid):
    lg = np.log1p(-np.exp2(-5.0 - np.arange(HC, dtype=np.float64)))
    n = np.arange(chunk, dtype=np.float64)
    diff = n[:, None] - n[None, :]
    decay = np.where(diff >= 0, np.exp(lg[:, None, None] * np.maximum(diff, 0.0)), 0.0)
    inner = np.exp(lg[:, None] * (n[None, :] + 1.0))
    kdec = np.where(n[None, :] < n_valid, np.exp(lg[:, None] * (n_valid - 1.0 - n[None, :])), 0.0)
    cdec = np.exp(lg * n_valid)
    inner_b = np.broadcast_to(inner[:, :, None], (HC, chunk, DV_C))
    kdec_b = np.broadcast_to(kdec[:, :, None], (HC, chunk, DK_C))
    cdec_b = np.broadcast_to(cdec[:, None, None], (HC, 1, DV_C))
    f = lambda a: jnp.asarray(np.ascontiguousarray(a), F32)
    return f(decay), f(inner_b), f(kdec_b), f(cdec_b)


def _layer_norm_rows(r, g, b):
    mu = jnp.mean(r, axis=-1, keepdims=True)
    xc = r - mu
    var = jnp.mean(xc * xc, axis=-1, keepdims=True)
    y = xc * lax.rsqrt(var + LN_EPS)
    if g is not None:
        y = y * g + b
    return y


def _ada_kernel(c_ref, w_ref, b_ref, o_ref):
    c = c_ref[...]
    s = (c * jax.nn.sigmoid(c)).astype(BF16)
    o_ref[...] = jnp.dot(s, w_ref[...].astype(BF16), preferred_element_type=F32) + b_ref[...]


def _ada(c_all, w_ada, b_ada):
    rows = c_all.shape[0]
    tn = 1024
    return pl.pallas_call(
        _ada_kernel,
        grid=(DEPTH, 6 * D_MODEL // tn),
        in_specs=[
            pl.BlockSpec((rows, D_MODEL), lambda l, j: (0, 0)),
            pl.BlockSpec((None, D_MODEL, tn), lambda l, j: (l, 0, j)),
            pl.BlockSpec((None, 1, tn), lambda l, j: (l, 0, j)),
        ],
        out_specs=pl.BlockSpec((None, rows, tn), lambda l, j: (l, 0, j)),
        out_shape=jax.ShapeDtypeStruct((DEPTH, rows, 6 * D_MODEL), F32),
        compiler_params=_params(2, 40),
        name="ada_mod",
    )(c_all, w_ada, b_ada.reshape(DEPTH, 1, 6 * D_MODEL))


def _in_kernel(x_ref, sc_ref, sh_ref, w_ref, o_ref, *rest, emit_bf16):
    h_ref = rest[-1]

    @pl.when(pl.program_id(2) == 0)
    def _():
        h_ref[...] = (x_ref[...] * (1.0 + sc_ref[...]) + sh_ref[...]).astype(BF16)

    w = w_ref[...]
    if emit_bf16:
        w = w.astype(BF16)
        rest[0][...] = w
    o_ref[...] = jnp.dot(h_ref[...], w, preferred_element_type=F32)


def _in_proj(x, sc, sh, w, layer, tm, emit_bf16):
    nb, rows, _ = x.shape
    mod_rows = sc.shape[1]
    tn = 512
    once = pl.Buffered(1)
    out_specs = [pl.BlockSpec((None, tm, tn), lambda b, t, j: (b, t, j))]
    out_shape = [jax.ShapeDtypeStruct((nb, rows, D_IN), F32)]
    if emit_bf16:
        assert nb * (rows // tm) == 1, "every weight tile must be visited exactly once"
        out_specs.append(pl.BlockSpec((None, D_MODEL, tn), lambda b, t, j: (0, 0, j)))
        out_shape.append(jax.ShapeDtypeStruct((1, D_MODEL, D_IN), BF16))
    return pl.pallas_call(
        functools.partial(_in_kernel, emit_bf16=emit_bf16),
        grid=(nb, rows // tm, D_IN // tn),
        in_specs=[
            pl.BlockSpec((None, tm, D_MODEL), lambda b, t, j: (b, t, 0), pipeline_mode=once),
            pl.BlockSpec((None, mod_rows, D_MODEL), lambda b, t, j: (b, 0, 0)),
            pl.BlockSpec((None, mod_rows, D_MODEL), lambda b, t, j: (b, 0, 0)),
            pl.BlockSpec((None, D_MODEL, tn), lambda b, t, j: (layer, 0, j)),
        ],
        out_specs=out_specs,
        out_shape=out_shape,
        scratch_shapes=[pltpu.VMEM((tm, D_MODEL), BF16)],
        compiler_params=_params(3, 48),
        name="in_proj",
    )(x, sc, sh, w)


def _conv_kernel(a_ref, ga_ref, init_ref, w_ref, cb_ref, g_ref, b_ref, y_ref, st_ref, ubuf,
                 *, tq, rb, n_valid_last, n_tiles):
    t = pl.program_id(1)

    @pl.when(t == 0)
    def _():
        ubuf[0:CONV_HALO, :] = init_ref[...]

    ga = ga_ref[...]
    ubuf[CONV_HALO:CONV_HALO + tq, :] = a_ref[...] * jax.nn.sigmoid(ga)
    first = CONV_HALO - (CONV_K - 1)
    for blk in range(tq // rb):
        r0 = blk * rb
        acc = jnp.broadcast_to(cb_ref[...], (rb, W_A))
        for k in range(CONV_K):
            acc = acc + w_ref[pl.ds(k, 1), :] * ubuf[r0 + first + k:r0 + first + k + rb, :]
        yv = _layer_norm_rows(acc, g_ref[...], b_ref[...])
        y_ref[r0:r0 + rb, :] = (yv * jax.nn.sigmoid(yv)).astype(BF16)

    @pl.when(t == n_tiles - 1)
    def _():
        st_ref[...] = ubuf[n_valid_last:n_valid_last + CONV_HALO, :]

    halo = ubuf[tq:tq + CONV_HALO, :]
    ubuf[0:CONV_HALO, :] = halo


def _conv_mixer(z, init, conv_w, conv_b, conv_g, conv_beta, layer, tq, n_valid_last):
    nb, rows, _ = z.shape
    n_tiles = rows // tq
    rb = min(tq, 32)
    vec = pl.BlockSpec((None, 1, W_A), lambda b, t: (layer, 0, 0))
    kern = functools.partial(_conv_kernel, tq=tq, rb=rb, n_valid_last=n_valid_last, n_tiles=n_tiles)
    return pl.pallas_call(
        kern,
        grid=(nb, n_tiles),
        in_specs=[
            pl.BlockSpec((None, tq, W_A), lambda b, t: (b, t, OFF_A // W_A)),
            pl.BlockSpec((None, tq, W_A), lambda b, t: (b, t, OFF_GA // W_A)),
            pl.BlockSpec((None, CONV_HALO, W_A), lambda b, t: (b, 0, 0)),
            pl.BlockSpec((None, CONV_K, W_A), lambda b, t: (layer, 0, 0)),
            vec, vec, vec,
        ],
        out_specs=[
            pl.BlockSpec((None, tq, W_A), lambda b, t: (b, t, 0)),
            pl.BlockSpec((None, CONV_HALO, W_A), lambda b, t: (b, 0, 0)),
        ],
        out_shape=[
            jax.ShapeDtypeStruct((nb, rows, W_A), BF16),
            jax.ShapeDtypeStruct((nb, CONV_HALO, W_A), F32),
        ],
        scratch_shapes=[pltpu.VMEM((CONV_HALO + tq, W_A), F32)],
        compiler_params=_params(2, 32),
        name="conv_mixer",
    )(z, z, init, conv_w, conv_b.reshape(DEPTH, 1, W_A), conv_g.reshape(DEPTH, 1, W_A),
      conv_beta.reshape(DEPTH, 1, W_A))


ATT_SCALE = HD_B ** -0.5


def _attn_prompt_kernel(q1, k1, v1, q2, k2, v2, q3, k3, v3, cos_ref, sin_ref,
                        yb_ref, kr1_ref, kr2_ref, kr3_ref, kr_s, o_s, l_s):
    qs, ks, vs = (q1, q2, q3), (k1, k2, k3), (v1, v2, v3)
    kr_outs = (kr1_ref, kr2_ref, kr3_ref)
    rows = 256

    for g in range(N_DIL):
        def rope_k(i, carry, g=g):
            s = pl.multiple_of(i * rows, rows)
            kr_s[g, pl.ds(s, rows), :] = _rope(ks[g][pl.ds(s, rows), :], cos_ref[pl.ds(s, rows), :],
                                               sin_ref[pl.ds(s, rows), :])
            return carry

        lax.fori_loop(0, SEQ // rows, rope_k, 0)
        win = min(DIL_CONFIGS[g][0], SEQ)
        kr_outs[g][...] = kr_s[g, SEQ - win:SEQ, :]

    qi2 = lax.broadcasted_iota(jnp.int32, (Q_BLOCK, 2 * Q_BLOCK), 0)
    kj2 = lax.broadcasted_iota(jnp.int32, (Q_BLOCK, 2 * Q_BLOCK), 1)
    band2 = (kj2 >= qi2) & (kj2 <= qi2 + Q_BLOCK)
    qi1 = lax.broadcasted_iota(jnp.int32, (Q_BLOCK, Q_BLOCK), 0)
    kj1 = lax.broadcasted_iota(jnp.int32, (Q_BLOCK, Q_BLOCK), 1)
    band1 = kj1 <= qi1

    for g, (window, dil) in enumerate(DIL_CONFIGS):
        assert window // dil == Q_BLOCK
        nblk = (SEQ // dil) // Q_BLOCK
        span = Q_BLOCK * dil

        def rows_of(start, dil=dil):
            return pl.ds(start, Q_BLOCK) if dil == 1 else pl.ds(start, Q_BLOCK, stride=dil)

        for r in range(dil):
            for n in range(nblk):
                cur = rows_of(r + n * span)
                q = _rope(qs[g][cur, :], cos_ref[cur, :], sin_ref[cur, :]).astype(BF16)
                k = kr_s[g, cur, :].astype(BF16)
                v = vs[g][cur, :].astype(BF16)
                band = band1
                if n > 0:
                    prev = rows_of(r + (n - 1) * span)
                    k = jnp.concatenate([kr_s[g, prev, :].astype(BF16), k], axis=0)
                    v = jnp.concatenate([vs[g][prev, :].astype(BF16), v], axis=0)
                    band = band2
                s = lax.dot_general(q, k, (((1,), (1,)), ((), ())), preferred_element_type=F32) * ATT_SCALE
                s = jnp.where(band, s, NEG_INF)
                m = jnp.max(s, axis=-1, keepdims=True)
                p = jnp.exp(s - m)
                den = jnp.sum(p, axis=-1, keepdims=True)
                o = jnp.dot(p.astype(BF16), v, preferred_element_type=F32)
                o_s[g, cur, :] = o / den
                l_s[g, cur, :] = jnp.broadcast_to(m + jnp.log(den), (Q_BLOCK, HD_B))

    def merge(i, carry):
        s = pl.multiple_of(i * rows, rows)
        sl = pl.ds(s, rows)
        l0, l1, l2 = l_s[0, sl, :], l_s[1, sl, :], l_s[2, sl, :]
        m = jnp.maximum(jnp.maximum(l0, l1), l2)
        w0, w1, w2 = jnp.exp(l0 - m), jnp.exp(l1 - m), jnp.exp(l2 - m)
        o = (w0 * o_s[0, sl, :] + w1 * o_s[1, sl, :] + w2 * o_s[2, sl, :]) / (w0 + w1 + w2)
        yb_ref[sl, :] = o.astype(BF16)
        return carry

    lax.fori_loop(0, SEQ // rows, merge, 0)


def _attn_prompt(z, cos, sin):
    nb = z.shape[0]
    head_cols = lambda off, g: (lambda b, h: (b, 0, off // HD_B + g * HB + h))
    in_specs = []
    for g in range(N_DIL):
        for off in (OFF_QB, OFF_KB, OFF_VB):
            in_specs.append(pl.BlockSpec((None, SEQ, HD_B), head_cols(off, g)))
    table = pl.BlockSpec((SEQ, HD_B), lambda b, h: (0, 0))
    wins = [min(w, SEQ) for w, _ in DIL_CONFIGS]
    return pl.pallas_call(
        _attn_prompt_kernel,
        grid=(nb, HB),
        in_specs=in_specs + [table, table],
        out_specs=[pl.BlockSpec((None, SEQ, HD_B), lambda b, h: (b, 0, h))]
        + [pl.BlockSpec((None, w, HD_B), lambda b, h: (b, 0, h)) for w in wins],
        out_shape=[jax.ShapeDtypeStruct((nb, SEQ, W_B), BF16)]
        + [jax.ShapeDtypeStruct((nb, w, W_B), F32) for w in wins],
        scratch_shapes=[pltpu.VMEM((N_DIL, SEQ, HD_B), F32)] * 3,
        compiler_params=_params(2, 48),
        name="attn_prompt",
    )(*([z] * 9), cos, sin)


def _attn_sample_kernel(z_ref, c1_ref, c2_ref, c3_ref, cos_ref, sin_ref, yb_ref, kn_ref, o_acc):
    caches = (c1_ref, c2_ref, c3_ref)
    cos, sin = cos_ref[...], sin_ref[...]
    tok = lax.broadcasted_iota(jnp.int32, (SAMPLE_ROWS, 1), 0)
    row = lax.broadcasted_iota(jnp.int32, (Q_BLOCK, 1), 0)
    o_acc[...] = jnp.zeros((SAMPLE_ROWS, W_B), F32)

    for h in range(HB):
        outs, lses = [], []
        for g, (window, dil) in enumerate(DIL_CONFIGS):
            col = (g * HB + h) * HD_B
            q = _rope(z_ref[:, OFF_QB + col:OFF_QB + col + HD_B], cos, sin)
            k_new = _rope(z_ref[:, OFF_KB + col:OFF_KB + col + HD_B], cos, sin)
            v_new = z_ref[:, OFF_VB + col:OFF_VB + col + HD_B]
            kn_ref[:, col:col + HD_B] = k_new
            length = min(window, PAST_LEN)
            nw = window // dil
            o_rows, l_rows = [], []
            for t in range(DEC_SEQ):
                q_t = q[t:t + 1, :]
                per_pos = 2 * HB
                if dil == 1:
                    base = length - nw
                    ok_c = row >= t
                    ok_n = tok <= t
                else:
                    base = length - nw * dil + t
                    ok_c = None
                    ok_n = tok == t
                k_c = caches[g][pl.ds(base * per_pos + h, nw, stride=per_pos * dil), :]
                v_c = caches[g][pl.ds(base * per_pos + HB + h, nw, stride=per_pos * dil), :]
                s_c = jnp.sum(k_c * q_t, axis=-1, keepdims=True) * ATT_SCALE
                if ok_c is not None:
                    s_c = jnp.where(ok_c, s_c, NEG_INF)
                s_n = jnp.sum(k_new * q_t, axis=-1, keepdims=True) * ATT_SCALE
                s_n = jnp.where(ok_n, s_n, NEG_INF)
                m = jnp.maximum(jnp.max(s_c, axis=0, keepdims=True), jnp.max(s_n, axis=0, keepdims=True))
                p_c = jnp.exp(s_c - m)
                p_n = jnp.exp(s_n - m)
                den = jnp.sum(p_c, axis=0, keepdims=True) + jnp.sum(p_n, axis=0, keepdims=True)
                o = (jnp.sum(p_c * v_c, axis=0, keepdims=True)
                     + jnp.sum(p_n * v_new, axis=0, keepdims=True)) / den
                o_rows.append(o)
                l_rows.append(m + jnp.log(den))
            outs.append(o_rows)
            lses.append(l_rows)
        for t in range(DEC_SEQ):
            l0, l1, l2 = lses[0][t], lses[1][t], lses[2][t]
            m = jnp.maximum(jnp.maximum(l0, l1), l2)
            w0, w1, w2 = jnp.exp(l0 - m), jnp.exp(l1 - m), jnp.exp(l2 - m)
            o = (w0 * outs[0][t] + w1 * outs[1][t] + w2 * outs[2][t]) / (w0 + w1 + w2)
            o_acc[t:t + 1, h * HD_B:(h + 1) * HD_B] = o
    yb_ref[...] = o_acc[...].astype(BF16)


def _attn_sample(z, caches, cos, sin, layer):
    nb = z.shape[0]
    cache_specs = [pl.BlockSpec((None, None, c.shape[2], HD_B), lambda b: (layer, b, 0, 0)) for c in caches]
    table = pl.BlockSpec((SAMPLE_ROWS, HD_B), lambda b: (0, 0))
    return pl.pallas_call(
        _attn_sample_kernel,
        grid=(nb,),
        in_specs=[pl.BlockSpec((None, SAMPLE_ROWS, D_IN), lambda b: (b, 0, 0))] + cache_specs + [table, table],
        out_specs=[
            pl.BlockSpec((None, SAMPLE_ROWS, W_B), lambda b: (b, 0, 0)),
            pl.BlockSpec((None, SAMPLE_ROWS, N_DIL * W_B), lambda b: (b, 0, 0)),
        ],
        out_shape=[
            jax.ShapeDtypeStruct((nb, SAMPLE_ROWS, W_B), BF16),
            jax.ShapeDtypeStruct((nb, SAMPLE_ROWS, N_DIL * W_B), F32),
        ],
        scratch_shapes=[pltpu.VMEM((SAMPLE_ROWS, W_B), F32)],
        compiler_params=_params(1, 40),
        name="attn_sample",
    )(z, *caches, cos, sin)


def _ret_kernel(q_ref, k_ref, v_ref, g_ref, cos_ref, sin_ref, dec_ref, inner_ref, kdec_ref, cdec_ref,
                s0_ref, y_ref, s_out_ref, state, *stage, rows):
    state[...] = s0_ref[...]
    chunk = RET_CHUNK

    def step(q, k, v, gate):
        qb = q.astype(BF16)
        vb = v.astype(BF16)
        sc = lax.dot_general(qb, k.astype(BF16), (((1,), (1,)), ((), ())),
                             preferred_element_type=F32) * dec_ref[...]
        o = jnp.dot(sc.astype(BF16), vb, preferred_element_type=F32)
        o = o + jnp.dot(qb, state[...].astype(BF16), preferred_element_type=F32) * inner_ref[...]
        kk = (k * kdec_ref[...]).astype(BF16)
        state[...] = state[...] * cdec_ref[...] + lax.dot_general(
            kk, vb, (((0,), (0,)), ((), ())), preferred_element_type=F32)
        gn = _layer_norm_rows(o, None, None)
        return (gate * jax.nn.sigmoid(gate)) * gn

    if rows % chunk == 0:
        def body(c, carry):
            sl = pl.ds(pl.multiple_of(c * chunk, chunk), chunk)
            cos, sin = cos_ref[sl, :], sin_ref[sl, :]
            q = _rope(q_ref[sl, :], cos, sin)
            k = _rope(k_ref[sl, :], cos, sin) * (DK_C ** -0.5)
            y_ref[sl, :] = step(q, k, v_ref[sl, :], g_ref[sl, :]).astype(BF16)
            return carry

        lax.fori_loop(0, rows // chunk, body, 0, unroll=4)
    else:
        qbuf, kbuf, vbuf, gbuf = stage
        qbuf[...] = jnp.zeros((chunk, DK_C), F32)
        kbuf[...] = jnp.zeros((chunk, DK_C), F32)
        vbuf[...] = jnp.zeros((chunk, DV_C), F32)
        gbuf[...] = jnp.zeros((chunk, DV_C), F32)
        cos, sin = cos_ref[...], sin_ref[...]
        qbuf[0:rows, :] = _rope(q_ref[...], cos, sin)
        kbuf[0:rows, :] = _rope(k_ref[...], cos, sin) * (DK_C ** -0.5)
        vbuf[0:rows, :] = v_ref[...]
        gbuf[0:rows, :] = g_ref[...]
        y = step(qbuf[...], kbuf[...], vbuf[...], gbuf[...])
        y_ref[...] = y[0:rows, :].astype(BF16)

    s_out_ref[...] = state[...]


def _retention(z, s0, cos, sin, tables, layer_state=None):
    nb, rows, _ = z.shape
    decay, inner_b, kdec_b, cdec_b = tables
    if layer_state is None:
        s0_spec = pl.BlockSpec((None, None, DK_C, DV_C), lambda b, h: (b, h, 0, 0))
    else:
        s0_spec = pl.BlockSpec((None, None, None, DK_C, DV_C), lambda b, h: (layer_state, b, h, 0, 0))
    staged = rows % RET_CHUNK != 0
    scratch = [pltpu.VMEM((DK_C, DV_C), F32)]
    if staged:
        scratch += [pltpu.VMEM((RET_CHUNK, DK_C), F32), pltpu.VMEM((RET_CHUNK, DK_C), F32),
                    pltpu.VMEM((RET_CHUNK, DV_C), F32), pltpu.VMEM((RET_CHUNK, DV_C), F32)]
    table_rows = cos.shape[0]
    return pl.pallas_call(
        functools.partial(_ret_kernel, rows=rows),
        grid=(nb, HC),
        in_specs=[
            pl.BlockSpec((None, rows, DK_C), lambda b, h: (b, 0, OFF_QC // DK_C + h)),
            pl.BlockSpec((None, rows, DK_C), lambda b, h: (b, 0, OFF_KC // DK_C + h)),
            pl.BlockSpec((None, rows, DV_C), lambda b, h: (b, 0, OFF_VC // DV_C + h)),
            pl.BlockSpec((None, rows, DV_C), lambda b, h: (b, 0, OFF_GC // DV_C + h)),
            pl.BlockSpec((table_rows, DK_C), lambda b, h: (0, 0)),
            pl.BlockSpec((table_rows, DK_C), lambda b, h: (0, 0)),
            pl.BlockSpec((None, RET_CHUNK, RET_CHUNK), lambda b, h: (h, 0, 0)),
            pl.BlockSpec((None, RET_CHUNK, DV_C), lambda b, h: (h, 0, 0)),
            pl.BlockSpec((None, RET_CHUNK, DK_C), lambda b, h: (h, 0, 0)),
            pl.BlockSpec((None, 1, DV_C), lambda b, h: (h, 0, 0)),
            s0_spec,
        ],
        out_specs=[
            pl.BlockSpec((None, rows, DV_C), lambda b, h: (b, 0, h)),
            pl.BlockSpec((None, None, DK_C, DV_C), lambda b, h: (b, h, 0, 0)),
        ],
        out_shape=[
            jax.ShapeDtypeStruct((nb, rows, W_C), BF16),
            jax.ShapeDtypeStruct((nb, HC, DK_C, DV_C), F32),
        ],
        scratch_shapes=scratch,
        compiler_params=_params(2, 40),
        name="retention",
    )(z, z, z, z, cos, sin, decay, inner_b, kdec_b, cdec_b, s0)


def _out_kernel(ya_ref, yb_ref, yc_ref, w_ref, x_ref, g1_ref, sc_ref, sh_ref, lg_ref, lb_ref,
                xo_ref, ho_ref, *wb_out, emit_bf16):
    wb_ref = w_ref
    if emit_bf16:
        wb_ref, = wb_out
        wb_ref[...] = w_ref[...].astype(BF16)

    acc = jnp.dot(ya_ref[...], wb_ref[0:W_A, :], preferred_element_type=F32)
    acc = acc + jnp.dot(yb_ref[...], wb_ref[W_A:W_A + W_B, :], preferred_element_type=F32)
    acc = acc + jnp.dot(yc_ref[...], wb_ref[W_A + W_B:, :], preferred_element_type=F32)
    r = ALPHA * x_ref[...] + g1_ref[...] * acc
    x1 = _layer_norm_rows(r, lg_ref[...], lb_ref[...])
    xo_ref[...] = x1
    ho_ref[...] = (x1 * (1.0 + sc_ref[...]) + sh_ref[...]).astype(BF16)


def _out_proj(ya, yb, yc, w, w_layer, x, g1, sc2, sh2, ln_g, ln_b, layer, tm, emit_bf16):
    nb, rows, _ = x.shape
    mod_rows = g1.shape[1]
    tile = lambda width: pl.BlockSpec((None, tm, width), lambda b, t: (b, t, 0))
    mod = pl.BlockSpec((None, mod_rows, D_MODEL), lambda b, t: (b, 0, 0))
    vec = pl.BlockSpec((None, 1, D_MODEL), lambda b, t: (layer, 0, 0))
    once = pl.Buffered(1)
    out_specs = [tile(D_MODEL), tile(D_MODEL)]
    out_shape = [jax.ShapeDtypeStruct((nb, rows, D_MODEL), F32), jax.ShapeDtypeStruct((nb, rows, D_MODEL), BF16)]
    if emit_bf16:
        assert nb * (rows // tm) == 1, "the weights must be visited exactly once"
        out_specs.append(pl.BlockSpec((None, D_MODEL, D_MODEL), lambda b, t: (0, 0, 0), pipeline_mode=once))
        out_shape.append(jax.ShapeDtypeStruct((1, D_MODEL, D_MODEL), BF16))
    return pl.pallas_call(
        functools.partial(_out_kernel, emit_bf16=emit_bf16),
        grid=(nb, rows // tm),
        in_specs=[
            tile(W_A), tile(W_B), tile(W_C),
            pl.BlockSpec((None, D_MODEL, D_MODEL), lambda b, t: (w_layer, 0, 0), pipeline_mode=once),
            tile(D_MODEL), mod, mod, mod, vec, vec,
        ],
        out_specs=out_specs,
        out_shape=out_shape,
        compiler_params=_params(2, 48),
        name="out_proj",
    )(ya, yb, yc, w, x, g1, sc2, sh2, ln_g.reshape(DEPTH, 1, D_MODEL), ln_b.reshape(DEPTH, 1, D_MODEL))


def _ffn_kernel(h_ref, wu_ref, wd_ref, x_ref, g2_ref, lg_ref, lb_ref, o_ref, *wb_out, n_f, emit_bf16):
    f = pl.program_id(2)

    @pl.when(f == 0)
    def _():
        o_ref[...] = jnp.zeros(o_ref.shape, F32)

    wu, wd = wu_ref, wd_ref
    if emit_bf16:
        wu, wd = wb_out
        wu[...] = wu_ref[...].astype(BF16)
        wd[...] = wd_ref[...].astype(BF16)
    u = jnp.dot(h_ref[...], wu[...], preferred_element_type=F32)
    a = jnp.square(jnp.maximum(u, 0.0)).astype(BF16)
    half = D_MODEL // 2
    for c in range(2):
        cols = slice(c * half, (c + 1) * half)
        o_ref[:, cols] += jnp.dot(a, wd[:, cols], preferred_element_type=F32)

    @pl.when(f == n_f - 1)
    def _():
        r = ALPHA * x_ref[...] + g2_ref[...] * o_ref[...]
        o_ref[...] = _layer_norm_rows(r, lg_ref[...], lb_ref[...])


def _ffn(h, w_up, w_down, w_layer, x, g2, ln_g, ln_b, layer, tm, tf, emit_bf16):
    nb, rows, _ = x.shape
    mod_rows = g2.shape[1]
    n_f = D_FF // tf
    once = pl.Buffered(1)
    vec = pl.BlockSpec((None, 1, D_MODEL), lambda b, t, f: (layer, 0, 0))
    out_specs = [pl.BlockSpec((None, tm, D_MODEL), lambda b, t, f: (b, t, 0), pipeline_mode=once)]
    out_shape = [jax.ShapeDtypeStruct((nb, rows, D_MODEL), F32)]
    if emit_bf16:
        assert nb * (rows // tm) == 1, "every weight tile must be visited exactly once"
        out_specs += [pl.BlockSpec((None, D_MODEL, tf), lambda b, t, f: (0, 0, f)),
                      pl.BlockSpec((None, tf, D_MODEL), lambda b, t, f: (0, f, 0))]
        out_shape += [jax.ShapeDtypeStruct((1, D_MODEL, D_FF), BF16), jax.ShapeDtypeStruct((1, D_FF, D_MODEL), BF16)]
    return pl.pallas_call(
        functools.partial(_ffn_kernel, n_f=n_f, emit_bf16=emit_bf16),
        grid=(nb, rows // tm, n_f),
        in_specs=[
            pl.BlockSpec((None, tm, D_MODEL), lambda b, t, f: (b, t, 0), pipeline_mode=once),
            pl.BlockSpec((None, D_MODEL, tf), lambda b, t, f: (w_layer, 0, f)),
            pl.BlockSpec((None, tf, D_MODEL), lambda b, t, f: (w_layer, f, 0)),
            pl.BlockSpec((None, tm, D_MODEL), lambda b, t, f: (b, t, 0), pipeline_mode=once),
            pl.BlockSpec((None, mod_rows, D_MODEL), lambda b, t, f: (b, 0, 0)),
            vec, vec,
        ],
        out_specs=out_specs,
        out_shape=out_shape,
        compiler_params=_params(3, 56),
        name="ffn",
    )(h, w_up, w_down, x, g2, ln_g.reshape(DEPTH, 1, D_MODEL), ln_b.reshape(DEPTH, 1, D_MODEL))


def kernel(x_prompt, x_sample, cache_conv, cache_win1, cache_win2, cache_win3, state_ret, c_prompt, c_sample,
           w_ada, b_ada, w_in, conv_w, conv_b, conv_ln_g, conv_ln_b, w_o, ln1_g, ln1_b, w_up, w_down,
           ln2_g, ln2_b):
    assert x_prompt.shape == (BATCH, SEQ, D_MODEL) and x_sample.shape == (DEC_BATCH, DEC_SEQ, D_MODEL)
    pad_tok = SAMPLE_ROWS - DEC_SEQ
    n_s = DEC_BATCH * SAMPLE_ROWS

    cos_p, sin_p = (jnp.asarray(t) for t in _rope_tables(np.arange(SEQ)))
    cos_s, sin_s = (jnp.asarray(t) for t in _rope_tables(PAST_LEN + np.arange(SAMPLE_ROWS)))
    tab_p = _retention_tables(RET_CHUNK, RET_CHUNK)
    tab_s = _retention_tables(RET_CHUNK, DEC_SEQ)

    c_all = jnp.concatenate([c_prompt, c_sample, jnp.zeros((4, D_MODEL), F32)], axis=0)
    mod = _ada(c_all, w_ada, b_ada)

    caches = tuple(c.reshape(DEPTH, DEC_BATCH, c.shape[2] * 2 * HB, HD_B)
                   for c in (cache_win1, cache_win2, cache_win3))
    conv_hist_s = jnp.pad(cache_conv, ((0, 0), (0, 0), (CONV_HALO - (CONV_K - 1), 0), (0, 0)))
    conv_hist_p = jnp.zeros((BATCH, CONV_HALO, W_A), F32)
    ret_zero = jnp.zeros((BATCH, HC, DK_C, DV_C), F32)

    xp = x_prompt
    xs = jnp.pad(x_sample, ((0, 0), (0, pad_tok), (0, 0))).reshape(1, n_s, D_MODEL)

    conv_p, conv_s, ret_p, ret_s = [], [], [], []
    win_p = [[] for _ in DIL_CONFIGS]
    new_s = [[] for _ in DIL_CONFIGS]
    for l in range(DEPTH):
        mods_p = [mod[l, :BATCH, i * D_MODEL:(i + 1) * D_MODEL][:, None, :] for i in range(6)]
        mods_s = [jnp.repeat(mod[l, BATCH:BATCH + DEC_BATCH, i * D_MODEL:(i + 1) * D_MODEL], SAMPLE_ROWS,
                             axis=0)[None] for i in range(6)]

        sh1, sc1, g1, sh2, sc2, g2 = mods_s
        zs, w_in_b = _in_proj(xs, sc1, sh1, w_in, l, tm=n_s, emit_bf16=True)
        zs3 = zs.reshape(DEC_BATCH, SAMPLE_ROWS, D_IN)
        ya, cst = _conv_mixer(zs3, conv_hist_s[l], conv_w, conv_b, conv_ln_g, conv_ln_b, l,
                              tq=SAMPLE_ROWS, n_valid_last=DEC_SEQ)
        yb, k_new = _attn_sample(zs3, caches, cos_s, sin_s, l)
        yc, ss = _retention(zs3, state_ret, cos_s, sin_s, tab_s, layer_state=l)
        flat = lambda a: a.reshape(1, n_s, a.shape[-1])
        x1, h2, w_o_b = _out_proj(flat(ya), flat(yb), flat(yc), w_o, l, xs, g1, sc2, sh2, ln1_g, ln1_b, l,
                                  tm=n_s, emit_bf16=True)
        xs, w_up_b, w_down_b = _ffn(h2, w_up, w_down, l, x1, g2, ln2_g, ln2_b, l, tm=n_s, tf=512, emit_bf16=True)
        conv_s.append(cst[:, CONV_HALO - (CONV_K - 1):])
        ret_s.append(ss)
        for g in range(N_DIL):
            kn = k_new[:, :DEC_SEQ, g * W_B:(g + 1) * W_B].reshape(DEC_BATCH, DEC_SEQ, HB, HD_B)
            vn = zs3[:, :DEC_SEQ, OFF_VB + g * W_B:OFF_VB + (g + 1) * W_B].reshape(DEC_BATCH, DEC_SEQ, HB, HD_B)
            new_s[g].append(jnp.stack([kn, vn], axis=2))

        sh1, sc1, g1, sh2, sc2, g2 = mods_p
        z, = _in_proj(xp, sc1, sh1, w_in_b, 0, tm=SEQ, emit_bf16=False)
        ya, cst = _conv_mixer(z, conv_hist_p, conv_w, conv_b, conv_ln_g, conv_ln_b, l, tq=256, n_valid_last=256)
        yb, kr1, kr2, kr3 = _attn_prompt(z, cos_p, sin_p)
        yc, sp = _retention(z, ret_zero, cos_p, sin_p, tab_p)
        x1, h2 = _out_proj(ya, yb, yc, w_o_b, 0, xp, g1, sc2, sh2, ln1_g, ln1_b, l, tm=512, emit_bf16=False)
        xp, = _ffn(h2, w_up_b, w_down_b, 0, x1, g2, ln2_g, ln2_b, l, tm=1024, tf=1024, emit_bf16=False)
        conv_p.append(cst[:, CONV_HALO - (CONV_K - 1):])
        ret_p.append(sp)
        for g, kr in enumerate((kr1, kr2, kr3)):
            n_win = kr.shape[1]
            v = z[:, SEQ - n_win:, OFF_VB + g * W_B:OFF_VB + (g + 1) * W_B]
            win_p[g].append(jnp.stack([kr.reshape(BATCH, n_win, HB, HD_B), v.reshape(BATCH, n_win, HB, HD_B)],
                                      axis=2))

    y_sample = xs.reshape(DEC_BATCH, SAMPLE_ROWS, D_MODEL)[:, :DEC_SEQ]
    wins_s = [jnp.concatenate([c[:, :, DEC_SEQ:], jnp.stack(new_s[g], 0)], axis=2)
              for g, c in enumerate((cache_win1, cache_win2, cache_win3))]
    return (xp, y_sample, jnp.stack(conv_p, 0), jnp.stack(conv_s, 0),
            jnp.stack(win_p[0], 0), wins_s[0], jnp.stack(win_p[1], 0), wins_s[1],
            jnp.stack(win_p[2], 0), wins_s[2], jnp.stack(ret_p, 0), jnp.stack(ret_s, 0))
```
